```python
import math
import jax
import jax.numpy as jnp
from jax import lax
import numpy as np

D_MODEL = 1024
BATCH = 8
SEQ = 2048
DEPTH = 1

N_META = 16
D_MIX = 2 * D_MODEL
SSD_WIDTH = D_MIX // 2
SSD_HEAD_DIM = 64
SSD_HEADS = SSD_WIDTH // SSD_HEAD_DIM
SSD_GROUPS = 2
SSD_HPG = SSD_HEADS // SSD_GROUPS
SSD_STATE = 128
SSD_CONV = 4
SSD_CHUNK = 128
SSD_CONV_DIM = SSD_WIDTH + 2 * SSD_GROUPS * SSD_STATE
LRU_WIDTH = D_MIX - SSD_WIDTH
LRU_BLOCKS = 16
LRU_BLOCK_W = LRU_WIDTH // LRU_BLOCKS
LRU_CONV = 4
LRU_C = 8.0
D_FF = -(-(8 * D_MODEL) // (3 * 256)) * 256
IN_COLS = SSD_WIDTH + SSD_CONV_DIM + SSD_HEADS + 2 * LRU_WIDTH
IN_SPLITS = [SSD_WIDTH, SSD_WIDTH + SSD_CONV_DIM, SSD_WIDTH + SSD_CONV_DIM + SSD_HEADS, SSD_WIDTH + SSD_CONV_DIM + SSD_HEADS + LRU_WIDTH]
EPS = 1e-6

kernel_name = 'hymba_ssd_rglru_hybrid_block'


def rmsnorm(x, w):
    xf = x.astype(jnp.float32)
    y = xf * lax.rsqrt(jnp.mean(xf * xf, axis=-1, keepdims=True) + EPS)
    return (y * w.astype(jnp.float32)).astype(x.dtype)


def causal_dwconv(x, w, b):
    k, c = w.shape
    y = lax.conv_general_dilated(x, w[:, None, :].astype(x.dtype), window_strides=(1,), padding=[(k - 1, 0)], dimension_numbers=('NWC', 'WIO', 'NWC'), feature_group_count=c)
    return y + b.astype(x.dtype)


def _to_chunks(t, pad):
    t = jnp.pad(t, [(0, 0), (pad, 0)] + [(0, 0)] * (t.ndim - 2))
    return t.reshape((t.shape[0], -1, SSD_CHUNK) + t.shape[2:])


def ssd_mixer(z, xbc, dt_raw, conv_w, conv_b, dt_bias, a_log, d_skip, norm_w):
    bsz, seqlen, _ = z.shape
    f32 = jnp.float32
    xbc = jax.nn.silu(causal_dwconv(xbc, conv_w, conv_b))
    xs, b_in, c_in = jnp.split(xbc, [SSD_WIDTH, SSD_WIDTH + SSD_GROUPS * SSD_STATE], axis=-1)
    dt = jax.nn.softplus(dt_raw.astype(f32) + dt_bias.astype(f32))
    a = -jnp.exp(a_log.astype(f32)).reshape(SSD_GROUPS, SSD_HPG)
    pad = (-seqlen) % SSD_CHUNK
    x_c = _to_chunks(xs.astype(f32).reshape(bsz, seqlen, SSD_GROUPS, SSD_HPG, SSD_HEAD_DIM), pad)
    b_c = _to_chunks(b_in.astype(f32).reshape(bsz, seqlen, SSD_GROUPS, SSD_STATE), pad)
    c_c = _to_chunks(c_in.astype(f32).reshape(bsz, seqlen, SSD_GROUPS, SSD_STATE), pad)
    dt_c = _to_chunks(dt.reshape(bsz, seqlen, SSD_GROUPS, SSD_HPG), pad)
    cs = jnp.cumsum(dt_c * a, axis=2)
    xdt = x_c * dt_c[..., None]
    causal = jnp.tril(jnp.ones((SSD_CHUNK, SSD_CHUNK), dtype=bool))
    seg = cs[:, :, :, None] - cs[:, :, None, :]
    lmat = jnp.exp(jnp.where(causal[:, :, None, None], seg, -jnp.inf))
    cb = jnp.einsum('bclgn,bcsgn->bclsg', c_c, b_c)
    y_diag = jnp.einsum('bclsgj,bcsgjp->bclgjp', cb[..., None] * lmat, xdt)
    decay_states = jnp.exp(cs[:, :, -1:] - cs)
    states = jnp.einsum('bclgn,bclgjp->bcgjpn', b_c, xdt * decay_states[..., None])
    chunk_decay = jnp.exp(cs[:, :, -1])

    def step(h, inp):
        s, d = inp
        return h * d[..., None, None] + s, h

    h0 = jnp.zeros((bsz, SSD_GROUPS, SSD_HPG, SSD_HEAD_DIM, SSD_STATE), f32)
    _, prev = lax.scan(step, h0, (jnp.moveaxis(states, 1, 0), jnp.moveaxis(chunk_decay, 1, 0)))
    prev = jnp.moveaxis(prev, 0, 1)
    y_off = jnp.einsum('bclgn,bcgjpn->bclgjp', c_c, prev) * jnp.exp(cs)[..., None]
    y = (y_diag + y_off).reshape(bsz, -1, SSD_WIDTH)[:, pad:]
    y = y + (xs.astype(f32).reshape(bsz, seqlen, SSD_HEADS, SSD_HEAD_DIM) * d_skip.astype(f32)[:, None]).reshape(bsz, seqlen, SSD_WIDTH)
    y = y.astype(z.dtype)
    g = (y * jax.nn.silu(z)).reshape(bsz, seqlen, SSD_GROUPS, SSD_WIDTH // SSD_GROUPS)
    return rmsnorm(g, norm_w.reshape(SSD_GROUPS, -1)).reshape(bsz, seqlen, SSD_WIDTH)


def rglru_mixer(gate, xr, conv_w, conv_b, wa, ba, wx, bx, lam, norm_w):
    bsz, seqlen, _ = xr.shape
    f32 = jnp.float32
    xr = causal_dwconv(xr, conv_w, conv_b)
    xb = xr.reshape(bsz, seqlen, LRU_BLOCKS, LRU_BLOCK_W)
    r = jax.nn.sigmoid(jnp.einsum('btni,nij->btnj', xb, wa).reshape(bsz, seqlen, LRU_WIDTH) + ba)
    i = jax.nn.sigmoid(jnp.einsum('btni,nij->btnj', xb, wx).reshape(bsz, seqlen, LRU_WIDTH) + bx)
    log_a = -LRU_C * r.astype(f32) * jax.nn.softplus(-lam.astype(f32))
    a = jnp.exp(log_a)
    u = jnp.sqrt(-jnp.expm1(2.0 * log_a)) * (i * xr).astype(f32)

    def combine(left, right):
        a1, b1 = left
        a2, b2 = right
        return a1 * a2, a2 * b1 + b2

    _, h = lax.associative_scan(combine, (a, u), axis=1)
    y = jax.nn.gelu(gate) * h.astype(gate.dtype)
    return rmsnorm(y, norm_w)


def setup_inputs(seed: int = 0) -> dict:
    key = jax.random.key(seed)
    ks = jax.random.split(key, 24)
    nrm = jax.random.normal
    dt0 = jnp.exp(jax.random.uniform(ks[6], (DEPTH, SSD_HEADS), minval=math.log(1e-3), maxval=math.log(1e-1)))
    a_base = jax.random.uniform(ks[15], (DEPTH, LRU_WIDTH), minval=0.9, maxval=0.999)
    s = a_base ** (1.0 / LRU_C)
    return {
        'x': nrm(ks[0], (BATCH, SEQ, D_MODEL), jnp.float32),
        'meta_tokens': nrm(ks[1], (N_META, D_MODEL), jnp.float32),
        'norm1_w': 1.0 + 0.02 * nrm(ks[2], (DEPTH, D_MODEL)),
        'w_in': nrm(ks[3], (DEPTH, D_MODEL, IN_COLS)) * D_MODEL ** -0.5,
        'ssd_conv_w': nrm(ks[4], (DEPTH, SSD_CONV, SSD_CONV_DIM)) * SSD_CONV ** -0.5,
        'ssd_conv_b': 0.02 * nrm(ks[5], (DEPTH, SSD_CONV_DIM)),
        'ssd_dt_bias': dt0 + jnp.log(-jnp.expm1(-dt0)),
        'ssd_a_log': jnp.log(jax.random.uniform(ks[7], (DEPTH, SSD_HEADS), minval=1.0, maxval=16.0)),
        'ssd_d': 1.0 + 0.1 * nrm(ks[8], (DEPTH, SSD_HEADS)),
        'ssd_norm_w': 1.0 + 0.02 * nrm(ks[9], (DEPTH, SSD_WIDTH)),
        'lru_conv_w': nrm(ks[10], (DEPTH, LRU_CONV, LRU_WIDTH)) * LRU_CONV ** -0.5,
        'lru_conv_b': 0.02 * nrm(ks[11], (DEPTH, LRU_WIDTH)),
        'lru_wa': nrm(ks[12], (DEPTH, LRU_BLOCKS, LRU_BLOCK_W, LRU_BLOCK_W)) * LRU_BLOCK_W ** -0.5,
        'lru_ba': 0.02 * nrm(ks[13], (DEPTH, LRU_WIDTH)),
        'lru_wx': nrm(ks[14], (DEPTH, LRU_BLOCKS, LRU_BLOCK_W, LRU_BLOCK_W)) * LRU_BLOCK_W ** -0.5,
        'lru_bx': 0.02 * nrm(ks[16], (DEPTH, LRU_WIDTH)),
        'lru_lambda': jnp.log(s) - jnp.log1p(-s),
        'lru_norm_w': 1.0 + 0.02 * nrm(ks[17], (DEPTH, LRU_WIDTH)),
        'w_out': nrm(ks[18], (DEPTH, D_MIX, D_MODEL)) * D_MIX ** -0.5,
        'norm2_w': 1.0 + 0.02 * nrm(ks[19], (DEPTH, D_MODEL)),
        'w_gate': nrm(ks[20], (DEPTH, D_MODEL, D_FF)) * D_MODEL ** -0.5,
        'w_up': nrm(ks[21], (DEPTH, D_MODEL, D_FF)) * D_MODEL ** -0.5,
        'w_down': nrm(ks[22], (DEPTH, D_FF, D_MODEL)) * D_FF ** -0.5,
        'final_norm_w': 1.0 + 0.02 * nrm(ks[23], (D_MODEL,)),
    }


def reference(x, meta_tokens, norm1_w, w_in, ssd_conv_w, ssd_conv_b, ssd_dt_bias, ssd_a_log, ssd_d, ssd_norm_w, lru_conv_w, lru_conv_b, lru_wa, lru_ba, lru_wx, lru_bx, lru_lambda, lru_norm_w, w_out, norm2_w, w_gate, w_up, w_down, final_norm_w):
    bsz = x.shape[0]
    meta = jnp.broadcast_to(meta_tokens.astype(x.dtype)[None], (bsz, N_META, D_MODEL))
    h = jnp.concatenate([meta, x], axis=1)
    for li in range(DEPTH):
        u = rmsnorm(h, norm1_w[li])
        proj = u @ w_in[li]
        z, xbc, dt_raw, g_lru, x_lru = jnp.split(proj, IN_SPLITS, axis=-1)
        y_ssd = ssd_mixer(z, xbc, dt_raw, ssd_conv_w[li], ssd_conv_b[li], ssd_dt_bias[li], ssd_a_log[li], ssd_d[li], ssd_norm_w[li])
        y_lru = rglru_mixer(g_lru, x_lru, lru_conv_w[li], lru_conv_b[li], lru_wa[li], lru_ba[li], lru_wx[li], lru_bx[li], lru_lambda[li], lru_norm_w[li])
        h = h + jnp.concatenate([y_ssd, y_lru], axis=-1) @ w_out[li]
        u = rmsnorm(h, norm2_w[li])
        h = h + (jax.nn.silu(u @ w_gate[li]) * (u @ w_up[li])) @ w_down[li]
    h = rmsnorm(h, final_norm_w)
    return h[:, N_META:]
```

```python
import functools

import jax
import jax.numpy as jnp
from jax import lax
from jax.experimental import pallas as pl
from jax.experimental.pallas import tpu as pltpu

D_MODEL = 1024
N_META = 16
SSD_WIDTH = 1024
SSD_HEAD_DIM = 64
SSD_HEADS = 16
SSD_GROUPS = 2
SSD_STATE = 128
SSD_CHUNK = 128
SSD_CONV_DIM = SSD_WIDTH + 2 * SSD_GROUPS * SSD_STATE
GROUP_W = SSD_WIDTH // SSD_GROUPS
LRU_WIDTH = 1024
LRU_BLOCKS = 16
LRU_BLOCK_W = 64
LRU_C = 8.0
D_FF = 2816
EPS = 1e-6

LANES = 128
SUBLANES = 8
MXU_DIM = 256
CONV_K = 4
TAIL = SUBLANES

C_Z = 0
C_XBC = C_Z + SSD_WIDTH
C_G = C_XBC + SSD_CONV_DIM
C_XL = C_G + LRU_WIDTH
C_DT = C_XL + LRU_WIDTH
IN_COLS_PAD = C_DT + LANES

GATE_BLOCKS = MXU_DIM // LRU_BLOCK_W
N_GATE_TILES = LRU_WIDTH // MXU_DIM

TILE_M = 256
VMEM_LIMIT = 58 * 1024 * 1024

F32 = jnp.float32
BF16 = jnp.bfloat16
HI = lax.Precision.HIGHEST


def _rms(x, w):
    ms = jnp.mean(x * x, axis=-1, keepdims=True)
    return x * lax.rsqrt(ms + EPS) * w


def _sigmoid(x):
    return 1.0 / (1.0 + jnp.exp(-x))


def _silu(x):
    return x * _sigmoid(x)


def _softplus(x):
    return jnp.maximum(x, 0.0) + jnp.log1p(jnp.exp(-jnp.abs(x)))


def _gelu_tanh(x):
    c = 0.7978845608028654
    return 0.5 * x * (1.0 + jnp.tanh(c * (x + 0.044715 * (x * x * x))))


def _conv(buf, w_ref, b_ref, tm):
    acc = b_ref[...] + w_ref[0:1, :] * buf[TAIL - 3:TAIL - 3 + tm, :]
    for k in range(1, CONV_K):
        acc = acc + w_ref[k:k + 1, :] * buf[TAIL - 3 + k:TAIL - 3 + k + tm, :]
    return acc


def _lru_scan(a, u, h_in):
    tm = a.shape[0]
    row = lax.broadcasted_iota(jnp.int32, a.shape, 0)
    acc_a, acc_h = a, u
    d = 1
    while d < tm:
        keep = row >= d
        sh_a = jnp.where(keep, pltpu.roll(acc_a, d, 0), 1.0)
        sh_h = jnp.where(keep, pltpu.roll(acc_h, d, 0), 0.0)
        acc_h = acc_a * sh_h + acc_h
        acc_a = acc_a * sh_a
        d *= 2
    return acc_h + acc_a * h_in


def _head_expand():
    r = lax.broadcasted_iota(jnp.int32, (LANES, SSD_WIDTH), 0)
    c = lax.broadcasted_iota(jnp.int32, (LANES, SSD_WIDTH), 1)
    return (r == c // SSD_HEAD_DIM).astype(F32)


def _ssd_chunk(xs, bm, cm, dt, a_row, s_ref, want_y):
    q = SSD_CHUNK
    expand = _head_expand()
    ri = lax.broadcasted_iota(jnp.int32, (q, q), 0)
    ci = lax.broadcasted_iota(jnp.int32, (q, q), 1)
    causal = ri >= ci
    ltri = causal.astype(F32)

    cs = jnp.dot(ltri, dt * a_row, precision=HI, preferred_element_type=F32)
    ecs = jnp.exp(cs)
    dt_full = jnp.dot(dt, expand, precision=HI, preferred_element_type=F32)
    ecs_full = jnp.dot(ecs, expand, precision=HI, preferred_element_type=F32)
    decay = jnp.exp(cs[q - 1:q, :] - cs)
    decay_full = jnp.dot(decay, expand, precision=HI, preferred_element_type=F32)
    xdt = xs * dt_full

    y = None
    if want_y:
        cs_t = cs.T
        lo_lanes = lax.broadcasted_iota(jnp.int32, (q, LANES), 1) < SSD_HEAD_DIM
        pieces = []
        for g in range(SSD_GROUPS):
            b_g = bm[:, g * SSD_STATE:(g + 1) * SSD_STATE].astype(BF16)
            c_g = cm[:, g * SSD_STATE:(g + 1) * SSD_STATE].astype(BF16)
            cb = lax.dot_general(c_g, b_g, (((1,), (1,)), ((), ())), preferred_element_type=F32)
            y_off = jnp.dot(c_g, s_ref[:, g * GROUP_W:(g + 1) * GROUP_W].astype(BF16),
                            preferred_element_type=F32)
            for k in range(GROUP_W // LANES):
                col = g * GROUP_W + k * LANES
                ms = []
                for h in (col // SSD_HEAD_DIM, col // SSD_HEAD_DIM + 1):
                    seg = cs[:, h:h + 1] - cs_t[h:h + 1, :]
                    ms.append(cb * jnp.exp(jnp.where(causal, seg, -jnp.inf)))
                m_pair = jnp.concatenate(ms, axis=1).astype(BF16)
                xp = xdt[:, col:col + LANES]
                rhs = jnp.concatenate([jnp.where(lo_lanes, xp, 0.0), jnp.where(lo_lanes, 0.0, xp)],
                                      axis=0).astype(BF16)
                y_d = jnp.dot(m_pair, rhs, preferred_element_type=F32)
                pieces.append(y_d + y_off[:, k * LANES:(k + 1) * LANES] * ecs_full[:, col:col + LANES])
        y = jnp.concatenate(pieces, axis=1)

    xdtd = (xdt * decay_full).astype(BF16)
    chunk_decay = ecs_full[q - 1:q, :]
    for g in range(SSD_GROUPS):
        b_g = bm[:, g * SSD_STATE:(g + 1) * SSD_STATE].astype(BF16)
        st = lax.dot_general(b_g, xdtd[:, g * GROUP_W:(g + 1) * GROUP_W], (((0,), (0,)), ((), ())),
                             preferred_element_type=F32)
        sl = slice(g * GROUP_W, (g + 1) * GROUP_W)
        s_ref[:, sl] = s_ref[:, sl] * chunk_decay[:, sl] + st
    return y


def _mixers(u, tm, valid, want_y, w_in, cw, cb, lcw, lcb, dtb, alog, wax, ba, bx, lam,
            s_ref, hl_ref, xbc_buf, xl_buf):
    def proj(lo, width):
        return jnp.dot(u, w_in[:, lo:lo + width], preferred_element_type=F32)

    xbc_buf[TAIL:TAIL + tm, :] = proj(C_XBC, SSD_CONV_DIM)
    xl_buf[TAIL:TAIL + tm, :] = proj(C_XL, LRU_WIDTH)
    dt_raw = proj(C_DT, LANES)

    xbc = _silu(_conv(xbc_buf, cw, cb, tm))
    xr = _conv(xl_buf, lcw, lcb, tm)
    xbc_buf[0:TAIL, :] = xbc_buf[tm:tm + TAIL, :]
    xl_buf[0:TAIL, :] = xl_buf[tm:tm + TAIL, :]

    dt = _softplus(dt_raw + dtb[...])
    if valid is not None:
        dt = jnp.where(valid, dt, 0.0)
    a_row = -jnp.exp(alog[...])
    xs = xbc[:, :SSD_WIDTH]
    bm = xbc[:, SSD_WIDTH:SSD_WIDTH + SSD_GROUPS * SSD_STATE]
    cm = xbc[:, SSD_WIDTH + SSD_GROUPS * SSD_STATE:]
    ys = []
    for c in range(tm // SSD_CHUNK):
        r = slice(c * SSD_CHUNK, (c + 1) * SSD_CHUNK)
        ys.append(_ssd_chunk(xs[r], bm[r], cm[r], dt[r], a_row, s_ref, want_y))

    xr_b = xr.astype(BF16)
    gates = [jnp.dot(xr_b[:, t * MXU_DIM:(t + 1) * MXU_DIM], wax[t], preferred_element_type=F32)
             for t in range(N_GATE_TILES)]
    ga = jnp.concatenate([g[:, :MXU_DIM] for g in gates], axis=1)
    gx = jnp.concatenate([g[:, MXU_DIM:] for g in gates], axis=1)
    r_gate = _sigmoid(ga + ba[...])
    i_gate = _sigmoid(gx + bx[...])
    log_a = -LRU_C * r_gate * _softplus(-lam[...])
    a = jnp.exp(log_a)
    t = jnp.tanh(log_a)
    mult = jnp.sqrt(-2.0 * t / (1.0 - t))
    uu = mult * (i_gate * xr)
    if valid is not None:
        uu = jnp.where(valid, uu, 0.0)
    h = _lru_scan(a, uu, hl_ref[0:1, :])
    hl_ref[...] = jnp.broadcast_to(h[tm - 1:tm, :], hl_ref.shape)
    if not want_y:
        return None
    return xs, jnp.concatenate(ys, axis=0), h


def _meta_kernel(x_ref, n1w, w_in, cw, cb, lcw, lcb, dtb, alog, wax, ba, bx, lam,
                 s_out, h_out, tail_out, ltail_out, xbc_buf, xl_buf):
    tm = SSD_CHUNK
    s_out[...] = jnp.zeros_like(s_out)
    h_out[...] = jnp.zeros_like(h_out)
    xbc_buf[0:TAIL, :] = jnp.zeros((TAIL, SSD_CONV_DIM), F32)
    xl_buf[0:TAIL, :] = jnp.zeros((TAIL, LRU_WIDTH), F32)
    valid = lax.broadcasted_iota(jnp.int32, (tm, 1), 0) >= tm - N_META
    u = _rms(x_ref[...], n1w[...]).astype(BF16)
    _mixers(u, tm, valid, False, w_in, cw, cb, lcw, lcb, dtb, alog, wax, ba, bx, lam,
            s_out, h_out, xbc_buf, xl_buf)
    tail_out[...] = xbc_buf[0:TAIL, :]
    ltail_out[...] = xl_buf[0:TAIL, :]


def _block_kernel(x_ref, n1w, w_in, cw, cb, lcw, lcb, dtb, alog, wax, ba, bx, lam,
                  dfull, snw, lnw, w_out, n2w, w_gu, w_down, fnw, s0, h0, tail0, ltail0,
                  o_ref, s_ref, hl_ref, xbc_buf, xl_buf):
    tm = TILE_M

    @pl.when(pl.program_id(1) == 0)
    def _():
        s_ref[...] = s0[...]
        hl_ref[...] = h0[...]
        xbc_buf[0:TAIL, :] = tail0[...]
        xl_buf[0:TAIL, :] = ltail0[...]

    x = x_ref[...]
    u = _rms(x, n1w[...]).astype(BF16)
    xs, y_scan, h_lru = _mixers(u, tm, None, True, w_in, cw, cb, lcw, lcb, dtb, alog, wax, ba, bx, lam,
                                s_ref, hl_ref, xbc_buf, xl_buf)

    z = jnp.dot(u, w_in[:, C_Z:C_Z + SSD_WIDTH], preferred_element_type=F32)
    gated = (y_scan + xs * dfull[...]) * _silu(z)
    y_ssd = jnp.concatenate(
        [_rms(gated[:, g * GROUP_W:(g + 1) * GROUP_W], snw[:, g * GROUP_W:(g + 1) * GROUP_W])
         for g in range(SSD_GROUPS)], axis=1)

    g_lru = jnp.dot(u, w_in[:, C_G:C_G + LRU_WIDTH], preferred_element_type=F32)
    y_lru = _rms(_gelu_tanh(g_lru) * h_lru, lnw[...])

    h1 = x + (jnp.dot(y_ssd.astype(BF16), w_out[0:SSD_WIDTH, :], preferred_element_type=F32)
              + jnp.dot(y_lru.astype(BF16), w_out[SSD_WIDTH:, :], preferred_element_type=F32))

    u2 = _rms(h1, n2w[...]).astype(BF16)
    gate = jnp.dot(u2, w_gu[:, :D_FF], preferred_element_type=F32)
    up = jnp.dot(u2, w_gu[:, D_FF:], preferred_element_type=F32)
    act = (_silu(gate) * up).astype(BF16)
    h2 = h1 + jnp.dot(act, w_down[...], preferred_element_type=F32)
    o_ref[...] = _rms(h2, fnw[...])


def _resident(shape):
    return pl.BlockSpec(shape, lambda *_: (0,) * len(shape), pipeline_mode=pl.Buffered(1))


def _row(v, width=None):
    v = v.astype(F32).reshape(1, -1)
    if width is not None and v.shape[1] < width:
        v = jnp.pad(v, ((0, 0), (0, width - v.shape[1])))
    return v


def _gate_tiles(wa, wx):
    def tiles(w):
        w = w.reshape(N_GATE_TILES, GATE_BLOCKS, LRU_BLOCK_W, LRU_BLOCK_W)
        eye = jnp.eye(GATE_BLOCKS, dtype=w.dtype)
        t = jnp.einsum('tbij,bc->tbicj', w, eye)
        return t.reshape(N_GATE_TILES, MXU_DIM, MXU_DIM)
    return jnp.concatenate([tiles(wa), tiles(wx)], axis=2).astype(BF16)


def kernel(x, meta_tokens, norm1_w, w_in, ssd_conv_w, ssd_conv_b, ssd_dt_bias, ssd_a_log, ssd_d, ssd_norm_w, lru_conv_w, lru_conv_b, lru_wa, lru_ba, lru_wx, lru_bx, lru_lambda, lru_norm_w, w_out, norm2_w, w_gate, w_up, w_down, final_norm_w):
    bsz, seq, d = x.shape
    assert d == D_MODEL and seq % TILE_M == 0 and norm1_w.shape[0] == 1
    li = 0

    wi = w_in[li]
    o_z, o_xbc, o_dt, o_g = SSD_WIDTH, SSD_WIDTH + SSD_CONV_DIM, SSD_WIDTH + SSD_CONV_DIM + SSD_HEADS, \
        SSD_WIDTH + SSD_CONV_DIM + SSD_HEADS + LRU_WIDTH
    w_in_r = jnp.concatenate(
        [wi[:, :o_z], wi[:, o_z:o_xbc], wi[:, o_dt:o_g], wi[:, o_g:], wi[:, o_xbc:o_dt],
         jnp.zeros((D_MODEL, LANES - SSD_HEADS), wi.dtype)], axis=1).astype(BF16)
    wax = _gate_tiles(lru_wa[li], lru_wx[li])
    w_out_b = w_out[li].astype(BF16)
    w_gu = jnp.concatenate([w_gate[li], w_up[li]], axis=1).astype(BF16)
    w_down_b = w_down[li].astype(BF16)

    n1w = _row(norm1_w[li])
    cw = ssd_conv_w[li].astype(F32)
    cb = _row(ssd_conv_b[li])
    lcw = lru_conv_w[li].astype(F32)
    lcb = _row(lru_conv_b[li])
    dtb = _row(ssd_dt_bias[li], LANES)
    alog = _row(ssd_a_log[li], LANES)
    dfull = _row(jnp.repeat(ssd_d[li], SSD_HEAD_DIM))
    snw = _row(ssd_norm_w[li])
    ba = _row(lru_ba[li])
    bx = _row(lru_bx[li])
    lam = _row(lru_lambda[li])
    lnw = _row(lru_norm_w[li])
    n2w = _row(norm2_w[li])
    fnw = _row(final_norm_w)

    mix_args = (n1w, w_in_r, cw, cb, lcw, lcb, dtb, alog, wax, ba, bx, lam)
    mix_specs = [_resident(a.shape) for a in mix_args]

    meta_pad = jnp.pad(meta_tokens.astype(F32), ((SSD_CHUNK - N_META, 0), (0, 0)))
    state_shapes = (
        jax.ShapeDtypeStruct((SSD_STATE, SSD_WIDTH), F32),
        jax.ShapeDtypeStruct((SUBLANES, LRU_WIDTH), F32),
        jax.ShapeDtypeStruct((TAIL, SSD_CONV_DIM), F32),
        jax.ShapeDtypeStruct((TAIL, LRU_WIDTH), F32),
    )
    s0, h0, tail0, ltail0 = pl.pallas_call(
        _meta_kernel,
        grid=(1,),
        in_specs=[_resident(meta_pad.shape)] + mix_specs,
        out_specs=[_resident(s.shape) for s in state_shapes],
        out_shape=state_shapes,
        scratch_shapes=[pltpu.VMEM((TAIL + SSD_CHUNK, SSD_CONV_DIM), F32),
                        pltpu.VMEM((TAIL + SSD_CHUNK, LRU_WIDTH), F32)],
        compiler_params=pltpu.CompilerParams(dimension_semantics=("arbitrary",),
                                             vmem_limit_bytes=VMEM_LIMIT),
        name="meta_prologue",
    )(meta_pad, *mix_args)

    tail_args = (dfull, snw, lnw, w_out_b, n2w, w_gu, w_down_b, fnw, s0, h0, tail0, ltail0)
    x_spec = pl.BlockSpec((None, TILE_M, D_MODEL), lambda b, i: (b, i, 0))
    out = pl.pallas_call(
        _block_kernel,
        grid=(bsz, seq // TILE_M),
        in_specs=[x_spec] + mix_specs + [_resident(a.shape) for a in tail_args],
        out_specs=pl.BlockSpec((None, TILE_M, D_MODEL), lambda b, i: (b, i, 0)),
        out_shape=jax.ShapeDtypeStruct((bsz, seq, D_MODEL), x.dtype),
        scratch_shapes=[pltpu.VMEM((SSD_STATE, SSD_WIDTH), F32),
                        pltpu.VMEM((SUBLANES, LRU_WIDTH), F32),
                        pltpu.VMEM((TAIL + TILE_M, SSD_CONV_DIM), F32),
                        pltpu.VMEM((TAIL + TILE_M, LRU_WIDTH), F32)],
        compiler_params=pltpu.CompilerParams(dimension_semantics=("arbitrary", "arbitrary"),
                                             vmem_limit_bytes=VMEM_LIMIT),
        name="hybrid_block",
    )(x.astype(F32), *mix_args, *tail_args)
    return out
```

```python
import functools

import jax
import jax.numpy as jnp
from jax import lax
from jax.experimental import pallas as pl
from jax.experimental.pallas import tpu as pltpu

D_MODEL = 1024
N_META = 16
SSD_WIDTH = 1024
SSD_HEAD_DIM = 64
SSD_HEADS = 16
SSD_GROUPS = 2
SSD_STATE = 128
SSD_CHUNK = 128
SSD_CONV_DIM = SSD_WIDTH + 2 * SSD_GROUPS * SSD_STATE
GROUP_W = SSD_WIDTH // SSD_GROUPS
LRU_WIDTH = 1024
LRU_BLOCKS = 16
LRU_BLOCK_W = 64
LRU_C = 8.0
D_FF = 2816
EPS = 1e-6

LANES = 128
SUBLANES = 8
MXU_DIM = 256
CONV_K = 4
TAIL = SUBLANES

C_Z = 0
C_XBC = C_Z + SSD_WIDTH
C_G = C_XBC + SSD_CONV_DIM
C_XL = C_G + LRU_WIDTH
C_DT = C_XL + LRU_WIDTH
IN_COLS_PAD = C_DT + LANES

GATE_BLOCKS = MXU_DIM // LRU_BLOCK_W
N_GATE_TILES = LRU_WIDTH // MXU_DIM

FF_CHUNK = MXU_DIM
N_FF_CHUNKS = D_FF // FF_CHUNK

TILE_M = 256
VMEM_LIMIT = 58 * 1024 * 1024

F32 = jnp.float32
BF16 = jnp.bfloat16
HI = lax.Precision.HIGHEST


def _rms(x, w):
    ms = jnp.mean(x * x, axis=-1, keepdims=True)
    return x * lax.rsqrt(ms + EPS) * w


def _sigmoid(x):
    return 1.0 / (1.0 + jnp.exp(-x))


def _silu(x):
    return x * _sigmoid(x)


def _softplus(x):
    return jnp.maximum(x, 0.0) + jnp.log1p(jnp.exp(-jnp.abs(x)))


def _gelu_tanh(x):
    c = 0.7978845608028654
    return 0.5 * x * (1.0 + jnp.tanh(c * (x + 0.044715 * (x * x * x))))


def _conv(buf, w_ref, b_ref, tm):
    acc = b_ref[...] + w_ref[0:1, :] * buf[TAIL - 3:TAIL - 3 + tm, :]
    for k in range(1, CONV_K):
        acc = acc + w_ref[k:k + 1, :] * buf[TAIL - 3 + k:TAIL - 3 + k + tm, :]
    return acc


def _lru_scan(a, u, h_in):
    tm = a.shape[0]
    row = lax.broadcasted_iota(jnp.int32, a.shape, 0)
    acc_a, acc_h = a, u
    d = 1
    while d < tm:
        keep = row >= d
        sh_a = jnp.where(keep, pltpu.roll(acc_a, d, 0), 1.0)
        sh_h = jnp.where(keep, pltpu.roll(acc_h, d, 0), 0.0)
        acc_h = acc_a * sh_h + acc_h
        acc_a = acc_a * sh_a
        d *= 2
    return acc_h + acc_a * h_in


def _head_expand():
    r = lax.broadcasted_iota(jnp.int32, (LANES, SSD_WIDTH), 0)
    c = lax.broadcasted_iota(jnp.int32, (LANES, SSD_WIDTH), 1)
    return (r == c // SSD_HEAD_DIM).astype(F32)


def _ssd_chunk(xs, bm, cm, dt, a_row, s_ref, want_y):
    q = SSD_CHUNK
    expand = _head_expand()
    ri = lax.broadcasted_iota(jnp.int32, (q, q), 0)
    ci = lax.broadcasted_iota(jnp.int32, (q, q), 1)
    causal = ri >= ci
    ltri = causal.astype(F32)

    cs = jnp.dot(ltri, dt * a_row, precision=HI, preferred_element_type=F32)
    ecs = jnp.exp(cs)
    dt_full = jnp.dot(dt, expand, precision=HI, preferred_element_type=F32)
    ecs_full = jnp.dot(ecs, expand, precision=HI, preferred_element_type=F32)
    decay = jnp.exp(cs[q - 1:q, :] - cs)
    decay_full = jnp.dot(decay, expand, precision=HI, preferred_element_type=F32)
    xdt = xs * dt_full

    y = None
    if want_y:
        cs_t = cs.T
        lo_lanes = lax.broadcasted_iota(jnp.int32, (q, LANES), 1) < SSD_HEAD_DIM
        pieces = []
        for g in range(SSD_GROUPS):
            b_g = bm[:, g * SSD_STATE:(g + 1) * SSD_STATE].astype(BF16)
            c_g = cm[:, g * SSD_STATE:(g + 1) * SSD_STATE].astype(BF16)
            cb = lax.dot_general(c_g, b_g, (((1,), (1,)), ((), ())), preferred_element_type=F32)
            y_off = jnp.dot(c_g, s_ref[:, g * GROUP_W:(g + 1) * GROUP_W].astype(BF16),
                            preferred_element_type=F32)
            for k in range(GROUP_W // LANES):
                col = g * GROUP_W + k * LANES
                ms = []
                for h in (col // SSD_HEAD_DIM, col // SSD_HEAD_DIM + 1):
                    seg = cs[:, h:h + 1] - cs_t[h:h + 1, :]
                    ms.append(cb * jnp.exp(jnp.where(causal, seg, -jnp.inf)))
                m_pair = jnp.concatenate(ms, axis=1).astype(BF16)
                xp = xdt[:, col:col + LANES]
                rhs = jnp.concatenate([jnp.where(lo_lanes, xp, 0.0), jnp.where(lo_lanes, 0.0, xp)],
                                      axis=0).astype(BF16)
                y_d = jnp.dot(m_pair, rhs, preferred_element_type=F32)
                pieces.append(y_d + y_off[:, k * LANES:(k + 1) * LANES] * ecs_full[:, col:col + LANES])
        y = jnp.concatenate(pieces, axis=1)

    xdtd = (xdt * decay_full).astype(BF16)
    chunk_decay = ecs_full[q - 1:q, :]
    for g in range(SSD_GROUPS):
        b_g = bm[:, g * SSD_STATE:(g + 1) * SSD_STATE].astype(BF16)
        st = lax.dot_general(b_g, xdtd[:, g * GROUP_W:(g + 1) * GROUP_W], (((0,), (0,)), ((), ())),
                             preferred_element_type=F32)
        sl = slice(g * GROUP_W, (g + 1) * GROUP_W)
        s_ref[:, sl] = s_ref[:, sl] * chunk_decay[:, sl] + st
    return y


def _project(u, tm, w_in, xbc_buf, xl_buf):
    def proj(lo, width):
        return jnp.dot(u, w_in[:, lo:lo + width], preferred_element_type=F32)

    xbc_buf[TAIL:TAIL + tm, :] = proj(C_XBC, SSD_CONV_DIM)
    xl_buf[TAIL:TAIL + tm, :] = proj(C_XL, LRU_WIDTH)
    return proj(C_DT, LANES)


def _conv_stage(dt_raw, tm, valid, cw, cb, lcw, lcb, dtb, alog, xbc_buf, xl_buf):
    xbc = _silu(_conv(xbc_buf, cw, cb, tm))
    xr = _conv(xl_buf, lcw, lcb, tm)
    xbc_buf[0:TAIL, :] = xbc_buf[tm:tm + TAIL, :]
    xl_buf[0:TAIL, :] = xl_buf[tm:tm + TAIL, :]
    dt = _softplus(dt_raw + dtb[...])
    if valid is not None:
        dt = jnp.where(valid, dt, 0.0)
    a_row = -jnp.exp(alog[...])
    return xbc, xr, dt, a_row


def _ssd_rows(xbc, dt, a_row, c, s_ref, want_y):
    r = slice(c * SSD_CHUNK, (c + 1) * SSD_CHUNK)
    xs = xbc[r, :SSD_WIDTH]
    bm = xbc[r, SSD_WIDTH:SSD_WIDTH + SSD_GROUPS * SSD_STATE]
    cm = xbc[r, SSD_WIDTH + SSD_GROUPS * SSD_STATE:]
    return _ssd_chunk(xs, bm, cm, dt[r], a_row, s_ref, want_y)


def _lru_gates(xr, wax):
    xr_b = xr.astype(BF16)
    gates = [jnp.dot(xr_b[:, t * MXU_DIM:(t + 1) * MXU_DIM], wax[t], preferred_element_type=F32)
             for t in range(N_GATE_TILES)]
    ga = jnp.concatenate([g[:, :MXU_DIM] for g in gates], axis=1)
    gx = jnp.concatenate([g[:, MXU_DIM:] for g in gates], axis=1)
    return ga, gx


def _lru_state(ga, gx, xr, valid, ba, bx, lam, hl_ref):
    tm = xr.shape[0]
    r_gate = _sigmoid(ga + ba[...])
    i_gate = _sigmoid(gx + bx[...])
    log_a = -LRU_C * r_gate * _softplus(-lam[...])
    a = jnp.exp(log_a)
    t = jnp.tanh(log_a)
    mult = jnp.sqrt(-2.0 * t / (1.0 - t))
    uu = mult * (i_gate * xr)
    if valid is not None:
        uu = jnp.where(valid, uu, 0.0)
    h = _lru_scan(a, uu, hl_ref[0:1, :])
    hl_ref[...] = jnp.broadcast_to(h[tm - 1:tm, :], hl_ref.shape)
    return h


def _meta_kernel(x_ref, n1w, w_in, cw, cb, lcw, lcb, dtb, alog, wax, ba, bx, lam,
                 s_out, h_out, tail_out, ltail_out, xbc_buf, xl_buf):
    tm = SSD_CHUNK
    s_out[...] = jnp.zeros_like(s_out)
    h_out[...] = jnp.zeros_like(h_out)
    xbc_buf[0:TAIL, :] = jnp.zeros((TAIL, SSD_CONV_DIM), F32)
    xl_buf[0:TAIL, :] = jnp.zeros((TAIL, LRU_WIDTH), F32)
    valid = lax.broadcasted_iota(jnp.int32, (tm, 1), 0) >= tm - N_META
    u = _rms(x_ref[...], n1w[...]).astype(BF16)
    dt_raw = _project(u, tm, w_in, xbc_buf, xl_buf)
    xbc, xr, dt, a_row = _conv_stage(dt_raw, tm, valid, cw, cb, lcw, lcb, dtb, alog, xbc_buf, xl_buf)
    _ssd_rows(xbc, dt, a_row, 0, s_out, False)
    ga, gx = _lru_gates(xr, wax)
    _lru_state(ga, gx, xr, valid, ba, bx, lam, h_out)
    tail_out[...] = xbc_buf[0:TAIL, :]
    ltail_out[...] = xl_buf[0:TAIL, :]


def _block_kernel(x_ref, n1w, w_in, cw, cb, lcw, lcb, dtb, alog, wax, ba, bx, lam,
                  dfull, snw, lnw, w_out, n2w, w_gu, w_down, fnw, s0, h0, tail0, ltail0,
                  o_ref, s_ref, hl_ref, xbc_buf, xl_buf, h1_buf, u2_buf, act_buf, *, tiles_per_row):
    tm = TILE_M
    n_chunks = tm // SSD_CHUNK
    step = pl.program_id(0)

    @pl.when(step == 0)
    def _():
        h1_buf[...] = jnp.zeros_like(h1_buf)
        u2_buf[...] = jnp.zeros_like(u2_buf)

    @pl.when(step % tiles_per_row == 0)
    def _():
        s_ref[...] = s0[...]
        hl_ref[...] = h0[...]
        xbc_buf[0:TAIL, :] = tail0[...]
        xl_buf[0:TAIL, :] = ltail0[...]

    x = x_ref[...]
    u = _rms(x, n1w[...]).astype(BF16)
    dt_raw = _project(u, tm, w_in, xbc_buf, xl_buf)
    z = jnp.dot(u, w_in[:, C_Z:C_Z + SSD_WIDTH], preferred_element_type=F32)
    g_lru = jnp.dot(u, w_in[:, C_G:C_G + LRU_WIDTH], preferred_element_type=F32)
    xbc, xr, dt, a_row = _conv_stage(dt_raw, tm, None, cw, cb, lcw, lcb, dtb, alog, xbc_buf, xl_buf)

    u2 = u2_buf[...]

    def ffn_up(j):
        cols = slice(j * FF_CHUNK, (j + 1) * FF_CHUNK)
        gate = jnp.dot(u2, w_gu[:, cols], preferred_element_type=F32)
        up = jnp.dot(u2, w_gu[:, D_FF + j * FF_CHUNK:D_FF + (j + 1) * FF_CHUNK], preferred_element_type=F32)
        act_buf[:, cols] = (_silu(gate) * up).astype(BF16)

    ys = []
    ga = gx = None
    for c in range(n_chunks):
        for j in range(c * N_FF_CHUNKS // n_chunks, (c + 1) * N_FF_CHUNKS // n_chunks):
            ffn_up(j)
        ys.append(_ssd_rows(xbc, dt, a_row, c, s_ref, True))
        if c == 0:
            ga, gx = _lru_gates(xr, wax)

    h2 = h1_buf[...] + jnp.dot(act_buf[...], w_down[...], preferred_element_type=F32)
    o_ref[...] = _rms(h2, fnw[...])

    h_lru = _lru_state(ga, gx, xr, None, ba, bx, lam, hl_ref)
    gated = (jnp.concatenate(ys, axis=0) + xbc[:, :SSD_WIDTH] * dfull[...]) * _silu(z)
    y_ssd = jnp.concatenate(
        [_rms(gated[:, g * GROUP_W:(g + 1) * GROUP_W], snw[:, g * GROUP_W:(g + 1) * GROUP_W])
         for g in range(SSD_GROUPS)], axis=1).astype(BF16)
    y_lru = _rms(_gelu_tanh(g_lru) * h_lru, lnw[...]).astype(BF16)
    h1 = x + (jnp.dot(y_ssd, w_out[0:SSD_WIDTH, :], preferred_element_type=F32)
              + jnp.dot(y_lru, w_out[SSD_WIDTH:, :], preferred_element_type=F32))
    h1_buf[...] = h1
    u2_buf[...] = _rms(h1, n2w[...]).astype(BF16)


def _resident(shape):
    return pl.BlockSpec(shape, lambda *_: (0,) * len(shape), pipeline_mode=pl.Buffered(1))


def _row(v, width=None):
    v = v.astype(F32).reshape(1, -1)
    if width is not None and v.shape[1] < width:
        v = jnp.pad(v, ((0, 0), (0, width - v.shape[1])))
    return v


def _gate_tiles(wa, wx):
    def tiles(w):
        w = w.reshape(N_GATE_TILES, GATE_BLOCKS, LRU_BLOCK_W, LRU_BLOCK_W)
        eye = jnp.eye(GATE_BLOCKS, dtype=w.dtype)
        t = jnp.einsum('tbij,bc->tbicj', w, eye)
        return t.reshape(N_GATE_TILES, MXU_DIM, MXU_DIM)
    return jnp.concatenate([tiles(wa), tiles(wx)], axis=2).astype(BF16)


def kernel(x, meta_tokens, norm1_w, w_in, ssd_conv_w, ssd_conv_b, ssd_dt_bias, ssd_a_log, ssd_d, ssd_norm_w, lru_conv_w, lru_conv_b, lru_wa, lru_ba, lru_wx, lru_bx, lru_lambda, lru_norm_w, w_out, norm2_w, w_gate, w_up, w_down, final_norm_w):
    bsz, seq, d = x.shape
    assert d == D_MODEL and seq % TILE_M == 0 and norm1_w.shape[0] == 1
    li = 0

    wi = w_in[li]
    o_z, o_xbc, o_dt, o_g = SSD_WIDTH, SSD_WIDTH + SSD_CONV_DIM, SSD_WIDTH + SSD_CONV_DIM + SSD_HEADS, \
        SSD_WIDTH + SSD_CONV_DIM + SSD_HEADS + LRU_WIDTH
    w_in_r = jnp.concatenate(
        [wi[:, :o_z], wi[:, o_z:o_xbc], wi[:, o_dt:o_g], wi[:, o_g:], wi[:, o_xbc:o_dt],
         jnp.zeros((D_MODEL, LANES - SSD_HEADS), wi.dtype)], axis=1).astype(BF16)
    wax = _gate_tiles(lru_wa[li], lru_wx[li])
    w_out_b = w_out[li].astype(BF16)
    w_gu = jnp.concatenate([w_gate[li], w_up[li]], axis=1).astype(BF16)
    w_down_b = w_down[li].astype(BF16)

    n1w = _row(norm1_w[li])
    cw = ssd_conv_w[li].astype(F32)
    cb = _row(ssd_conv_b[li])
    lcw = lru_conv_w[li].astype(F32)
    lcb = _row(lru_conv_b[li])
    dtb = _row(ssd_dt_bias[li], LANES)
    alog = _row(ssd_a_log[li], LANES)
    dfull = _row(jnp.repeat(ssd_d[li], SSD_HEAD_DIM))
    snw = _row(ssd_norm_w[li])
    ba = _row(lru_ba[li])
    bx = _row(lru_bx[li])
    lam = _row(lru_lambda[li])
    lnw = _row(lru_norm_w[li])
    n2w = _row(norm2_w[li])
    fnw = _row(final_norm_w)

    mix_args = (n1w, w_in_r, cw, cb, lcw, lcb, dtb, alog, wax, ba, bx, lam)
    mix_specs = [_resident(a.shape) for a in mix_args]

    meta_pad = jnp.pad(meta_tokens.astype(F32), ((SSD_CHUNK - N_META, 0), (0, 0)))
    state_shapes = (
        jax.ShapeDtypeStruct((SSD_STATE, SSD_WIDTH), F32),
        jax.ShapeDtypeStruct((SUBLANES, LRU_WIDTH), F32),
        jax.ShapeDtypeStruct((TAIL, SSD_CONV_DIM), F32),
        jax.ShapeDtypeStruct((TAIL, LRU_WIDTH), F32),
    )
    s0, h0, tail0, ltail0 = pl.pallas_call(
        _meta_kernel,
        grid=(1,),
        in_specs=[_resident(meta_pad.shape)] + mix_specs,
        out_specs=[_resident(s.shape) for s in state_shapes],
        out_shape=state_shapes,
        scratch_shapes=[pltpu.VMEM((TAIL + SSD_CHUNK, SSD_CONV_DIM), F32),
                        pltpu.VMEM((TAIL + SSD_CHUNK, LRU_WIDTH), F32)],
        compiler_params=pltpu.CompilerParams(dimension_semantics=("arbitrary",),
                                             vmem_limit_bytes=VMEM_LIMIT),
        name="meta_prologue",
    )(meta_pad, *mix_args)

    tail_args = (dfull, snw, lnw, w_out_b, n2w, w_gu, w_down_b, fnw, s0, h0, tail0, ltail0)
    tiles_per_row = seq // TILE_M
    n_tiles = bsz * tiles_per_row

    def in_tile(s):
        t = jnp.minimum(s, n_tiles - 1)
        return (t // tiles_per_row, t % tiles_per_row, 0)

    def out_tile(s):
        t = jnp.maximum(s - 1, 0)
        return (t // tiles_per_row, t % tiles_per_row, 0)

    out = pl.pallas_call(
        functools.partial(_block_kernel, tiles_per_row=tiles_per_row),
        grid=(n_tiles + 1,),
        in_specs=[pl.BlockSpec((None, TILE_M, D_MODEL), in_tile)] + mix_specs
        + [_resident(a.shape) for a in tail_args],
        out_specs=pl.BlockSpec((None, TILE_M, D_MODEL), out_tile),
        out_shape=jax.ShapeDtypeStruct((bsz, seq, D_MODEL), x.dtype),
        scratch_shapes=[pltpu.VMEM((SSD_STATE, SSD_WIDTH), F32),
                        pltpu.VMEM((SUBLANES, LRU_WIDTH), F32),
                        pltpu.VMEM((TAIL + TILE_M, SSD_CONV_DIM), F32),
                        pltpu.VMEM((TAIL + TILE_M, LRU_WIDTH), F32),
                        pltpu.VMEM((TILE_M, D_MODEL), F32),
                        pltpu.VMEM((TILE_M, D_MODEL), BF16),
                        pltpu.VMEM((TILE_M, D_FF), BF16)],
        compiler_params=pltpu.CompilerParams(dimension_semantics=("arbitrary",),
                                             vmem_limit_bytes=VMEM_LIMIT),
        name="hybrid_block",
    )(x.astype(F32), *mix_args, *tail_args)
    return out
```
